```python
import jax, jax.numpy as jnp
from jax import lax
import numpy as np

D_MODEL = 4096
BATCH = 2
SEQ = 8192
DEPTH = 4
DEC_BATCH = 1
DEC_SEQ = 8192
PAST_LEN = 128

N_MIXERS = 3
GRID_W = 64
CHUNK = 128
A_GROUPS = 16
A_WIDTH = D_MODEL
POOL_WINDOWS = (2, 4, 8, 16)
B_GROUPS = 4
B_GW = D_MODEL // B_GROUPS
C_HEADS = 32
C_HEAD_DIM = D_MODEL // C_HEADS
WIN_ROWS_MAX = 8
WIN_COLS = 16
D_FF = 4 * D_MODEL
LN_EPS = 1e-5
ALPHA = (2 * DEPTH) ** 0.25
BETA = (8 * DEPTH) ** -0.25
N_A = (DEPTH + 2) // 3
N_B = (DEPTH + 1) // 3
N_C = DEPTH // 3
NEG = -1e30

kernel_name = "hybrid_gmlp_pool_natten_encoder"


def layer_norm(x, g, b):
    xf = x.astype(jnp.float32)
    mu = jnp.mean(xf, axis=-1, keepdims=True)
    xc = xf - mu
    var = jnp.mean(xc * xc, axis=-1, keepdims=True)
    y = xc * lax.rsqrt(var + LN_EPS) * g.astype(jnp.float32) + b.astype(jnp.float32)
    return y.astype(x.dtype)


def spatial_gating_mixer(x, w_in, ln_g, ln_b, w_s, b_s, w_out):
    bsz, t, _ = x.shape
    z = jax.nn.gelu(x @ w_in, approximate=False)
    u, v = jnp.split(z, 2, axis=-1)
    v = layer_norm(v, ln_g, ln_b)
    v = v.reshape(bsz, t // CHUNK, CHUNK, A_GROUPS, A_WIDTH // A_GROUPS)
    sv = jnp.einsum('gpq,bnqgc->bnpgc', w_s, v) + b_s.T[None, None, :, :, None]
    return (u * sv.reshape(bsz, t, A_WIDTH)) @ w_out


def pool_mixer(x, w_in, w_grp, scale, w_out):
    bsz, t, _ = x.shape
    h = (x @ w_in).reshape(bsz, t, B_GROUPS, B_GW)
    pos = jnp.arange(t)
    outs = []
    for g, w in enumerate(POOL_WINDOWS):
        hg = h[:, :, g].astype(jnp.float32)
        csum = jnp.concatenate([jnp.zeros((bsz, 1, B_GW), jnp.float32),
                                jnp.cumsum(hg, axis=1)], axis=1)
        left = w // 2
        right = w - 1 - left
        lo = jnp.clip(pos - left, 0, t)
        hi = jnp.clip(pos + right + 1, 0, t)
        cnt = (hi - lo).astype(jnp.float32)[None, :, None]
        mean = (csum[:, hi] - csum[:, lo]) / cnt
        outs.append((mean - hg).astype(x.dtype))
    p = jnp.stack(outs, axis=2)
    m = jnp.einsum('btgc,gcd->btgd', p, w_grp) * scale.reshape(B_GROUPS, B_GW)
    return m.reshape(bsz, t, D_MODEL) @ w_out


def neighbourhood_attention_mixer(x, w_qkv, rpb, w_out):
    bsz, t, _ = x.shape
    rows = t // GRID_W
    kh = min(WIN_ROWS_MAX, rows)
    qkv = (x @ w_qkv).reshape(bsz, rows, GRID_W, 3, C_HEADS, C_HEAD_DIM)
    q = qkv[:, :, :, 0] * (C_HEAD_DIM ** -0.5)
    k = qkv[:, :, :, 1]
    v = qkv[:, :, :, 2]
    n_cb = GRID_W // WIN_COLS
    kspan = 2 * WIN_COLS
    qc = np.arange(GRID_W).reshape(n_cb, WIN_COLS)
    kc0 = np.clip(np.arange(n_cb) * WIN_COLS - WIN_COLS // 2, 0, GRID_W - kspan)
    kc = kc0[:, None] + np.arange(kspan)[None, :]
    cstart = np.clip(qc - WIN_COLS // 2, 0, GRID_W - WIN_COLS)
    col_ok = (kc[:, None, :] >= cstart[:, :, None]) & (kc[:, None, :] < cstart[:, :, None] + WIN_COLS)
    col_off = np.clip(kc[:, None, :] - qc[:, :, None] + WIN_COLS - 1, 0, 2 * WIN_COLS - 2)
    rpb_cols = rpb[:, :, col_off]
    k_cb = k[:, :, kc]
    v_cb = v[:, :, kc]
    q_cb = q.reshape(bsz, rows, n_cb, WIN_COLS, C_HEADS, C_HEAD_DIM)
    mask = jnp.asarray(col_ok)[None, None, :, :, None, :]

    def row_step(r):
        rs = jnp.clip(r - kh // 2, 0, rows - kh)
        kr = lax.dynamic_slice_in_dim(k_cb, rs, kh, axis=1)
        vr = lax.dynamic_slice_in_dim(v_cb, rs, kh, axis=1)
        qr = lax.dynamic_index_in_dim(q_cb, r, axis=1, keepdims=False)
        s = jnp.einsum('bjqhd,bijkhd->bhjqik', qr, kr).astype(jnp.float32)
        row_off = rs + jnp.arange(kh) - r + WIN_ROWS_MAX - 1
        bias = jnp.take(rpb_cols, row_off, axis=1).transpose(0, 2, 3, 1, 4)
        s = jnp.where(mask, s + bias.astype(jnp.float32)[None], NEG)
        shp = s.shape
        p = jax.nn.softmax(s.reshape(shp[:4] + (kh * kspan,)), axis=-1).reshape(shp)
        return jnp.einsum('bhjqik,bijkhd->bjqhd', p.astype(vr.dtype), vr)

    out = lax.map(row_step, jnp.arange(rows))
    out = jnp.moveaxis(out, 0, 1).reshape(bsz, t, D_MODEL)
    return out @ w_out


def sq_relu_mlp(x, w1, w2):
    h = jax.nn.relu(x @ w1)
    return (h * h) @ w2


def trunk(x, a_w_in, a_ln_g, a_ln_b, a_w_s, a_b_s, a_w_out,
          b_w_in, b_w_grp, b_scale, b_w_out,
          c_w_qkv, c_rpb, c_w_out,
          ln1_g, ln1_b, ffn_w1, ffn_w2, ln2_g, ln2_b):
    for i in range(DEPTH):
        kind = i % N_MIXERS
        j = i // N_MIXERS
        if kind == 0:
            m = spatial_gating_mixer(x, a_w_in[j], a_ln_g[j], a_ln_b[j], a_w_s[j], a_b_s[j], a_w_out[j])
        elif kind == 1:
            m = pool_mixer(x, b_w_in[j], b_w_grp[j], b_scale[j], b_w_out[j])
        else:
            m = neighbourhood_attention_mixer(x, c_w_qkv[j], c_rpb[j], c_w_out[j])
        x = layer_norm(ALPHA * x + m, ln1_g[i], ln1_b[i])
        x = layer_norm(ALPHA * x + sq_relu_mlp(x, ffn_w1[i], ffn_w2[i]), ln2_g[i], ln2_b[i])
    return x


def setup_inputs(seed: int = 0) -> dict:
    key = jax.random.key(seed)
    ks = jax.random.split(key, 24)
    f32 = jnp.float32

    def nrm(k, shape, scale):
        return jax.random.normal(k, shape, f32) * scale

    return {
        "x_prompt": nrm(ks[0], (BATCH, SEQ, D_MODEL), 1.0),
        "x_sample": nrm(ks[1], (DEC_BATCH, DEC_SEQ, D_MODEL), 1.0),
        "a_w_in": nrm(ks[2], (N_A, D_MODEL, 2 * A_WIDTH), D_MODEL ** -0.5),
        "a_ln_g": 1.0 + nrm(ks[3], (N_A, A_WIDTH), 0.1),
        "a_ln_b": nrm(ks[4], (N_A, A_WIDTH), 0.02),
        "a_w_s": nrm(ks[5], (N_A, A_GROUPS, CHUNK, CHUNK), CHUNK ** -0.5),
        "a_b_s": 1.0 + nrm(ks[6], (N_A, A_GROUPS, CHUNK), 0.1),
        "a_w_out": nrm(ks[7], (N_A, A_WIDTH, D_MODEL), BETA * A_WIDTH ** -0.5),
        "b_w_in": nrm(ks[8], (N_B, D_MODEL, D_MODEL), D_MODEL ** -0.5),
        "b_w_grp": nrm(ks[9], (N_B, B_GROUPS, B_GW, B_GW), B_GW ** -0.5),
        "b_scale": 1.0 + nrm(ks[10], (N_B, D_MODEL), 0.1),
        "b_w_out": nrm(ks[11], (N_B, D_MODEL, D_MODEL), BETA * D_MODEL ** -0.5),
        "c_w_qkv": nrm(ks[12], (N_C, D_MODEL, 3 * D_MODEL), D_MODEL ** -0.5),
        "c_rpb": nrm(ks[13], (N_C, C_HEADS, 2 * WIN_ROWS_MAX - 1, 2 * WIN_COLS - 1), 0.1),
        "c_w_out": nrm(ks[14], (N_C, D_MODEL, D_MODEL), BETA * D_MODEL ** -0.5),
        "ln1_g": 1.0 + nrm(ks[15], (DEPTH, D_MODEL), 0.1),
        "ln1_b": nrm(ks[16], (DEPTH, D_MODEL), 0.02),
        "ffn_w1": nrm(ks[17], (DEPTH, D_MODEL, D_FF), D_MODEL ** -0.5),
        "ffn_w2": nrm(ks[18], (DEPTH, D_FF, D_MODEL), BETA * D_FF ** -0.5),
        "ln2_g": 1.0 + nrm(ks[19], (DEPTH, D_MODEL), 0.1),
        "ln2_b": nrm(ks[20], (DEPTH, D_MODEL), 0.02),
    }


def reference(x_prompt, x_sample, a_w_in, a_ln_g, a_ln_b, a_w_s, a_b_s, a_w_out,
              b_w_in, b_w_grp, b_scale, b_w_out, c_w_qkv, c_rpb, c_w_out,
              ln1_g, ln1_b, ffn_w1, ffn_w2, ln2_g, ln2_b):
    params = (a_w_in, a_ln_g, a_ln_b, a_w_s, a_b_s, a_w_out,
              b_w_in, b_w_grp, b_scale, b_w_out,
              c_w_qkv, c_rpb, c_w_out,
              ln1_g, ln1_b, ffn_w1, ffn_w2, ln2_g, ln2_b)
    y_prompt = trunk(x_prompt, *params)
    y_sample = trunk(x_sample, *params)
    return (y_prompt, y_sample)
```

```python
import functools

import numpy as np
import jax
import jax.numpy as jnp
from jax import lax
from jax.experimental import pallas as pl
from jax.experimental.pallas import tpu as pltpu

F32 = jnp.float32
BF16 = jnp.bfloat16

N_MIXERS = 3
GRID_W = 64
POOL_WINDOWS = (2, 4, 8, 16)
LN_EPS = 1e-5
NEG = -1e30

V7X_VMEM_BYTES = 64 * 1024 * 1024
V7X_SUBLANES = 8
V7X_LANES = 128
COMPILER_SCRATCH_BYTES = 12 * 1024 * 1024

MM_TM = 1024
MM_TN = 1024
ACC_TM = 1024
ACC_TN = 2048
ACC_TK = 1024
LN_ROWS = 256
GATE_ROWS = 512
POOL_ROWS = 256
POOL_HALO = 8
ATTN_HEADS_PER_STEP = 2


def _tile(n, pref, align):
    if n <= pref:
        return n
    t = (pref // align) * align
    while t >= align:
        if n % t == 0:
            return t
        t -= align
    raise ValueError(f"no {align}-aligned tile of {n} below {pref}")


def _nbytes(shape, dtype):
    return int(np.prod(shape)) * jnp.dtype(dtype).itemsize


def _params(semantics, blocks, temps=0):
    need = 2 * sum(_nbytes(s, d) for s, d in blocks) + temps + COMPILER_SCRATCH_BYTES
    return pltpu.CompilerParams(
        dimension_semantics=semantics,
        vmem_limit_bytes=min(need, V7X_VMEM_BYTES),
    )


def _mm_kernel(x_ref, w_ref, o_ref, *, act, scaled_col_tiles, col_scale):
    acc = jnp.dot(x_ref[...], w_ref[...], preferred_element_type=F32)
    if act == "gelu":
        acc = 0.5 * acc * (1.0 + lax.erf(acc * np.float32(np.sqrt(0.5))))
    elif act == "relu2":
        r = jnp.maximum(acc, 0.0)
        acc = r * r
    elif act == "colscale":
        s = jnp.where(pl.program_id(1) < scaled_col_tiles, col_scale, 1.0).astype(F32)
        acc = acc * s
    o_ref[...] = acc.astype(o_ref.dtype)


def _matmul(x, w, out_dtype, act=None, scaled_cols=0, col_scale=1.0, name="mm"):
    m, k = x.shape
    n = w.shape[1]
    tm = _tile(m, MM_TM, V7X_SUBLANES * 2)
    tn = _tile(n, MM_TN, V7X_LANES)
    if scaled_cols:
        assert scaled_cols % tn == 0
    kern = functools.partial(_mm_kernel, act=act, scaled_col_tiles=scaled_cols // tn,
                             col_scale=col_scale)
    return pl.pallas_call(
        kern,
        grid=(m // tm, n // tn),
        in_specs=[pl.BlockSpec((tm, k), lambda i, j: (i, 0)),
                  pl.BlockSpec((k, tn), lambda i, j: (0, j))],
        out_specs=pl.BlockSpec((tm, tn), lambda i, j: (i, j)),
        out_shape=jax.ShapeDtypeStruct((m, n), out_dtype),
        compiler_params=_params(("parallel", "arbitrary"),
                                [((tm, k), x.dtype), ((k, tn), w.dtype), ((tm, tn), out_dtype)],
                                temps=_nbytes((tm, tn), F32)),
        name=name,
    )(x, w)


def _mm_acc_kernel(x_ref, w_ref, o_ref):
    @pl.when(pl.program_id(2) == 0)
    def _():
        o_ref[...] = jnp.zeros_like(o_ref)

    o_ref[...] += jnp.dot(x_ref[...], w_ref[...], preferred_element_type=F32)


def _matmul_acc(x, w, name="mm_acc"):
    m, k = x.shape
    n = w.shape[1]
    tm = _tile(m, ACC_TM, V7X_SUBLANES * 2)
    tn = _tile(n, ACC_TN, V7X_LANES)
    tk = _tile(k, ACC_TK, V7X_LANES)
    return pl.pallas_call(
        _mm_acc_kernel,
        grid=(m // tm, n // tn, k // tk),
        in_specs=[pl.BlockSpec((tm, tk), lambda i, j, l: (i, l)),
                  pl.BlockSpec((tk, tn), lambda i, j, l: (l, j))],
        out_specs=pl.BlockSpec((tm, tn), lambda i, j, l: (i, j)),
        out_shape=jax.ShapeDtypeStruct((m, n), F32),
        compiler_params=_params(("parallel", "parallel", "arbitrary"),
                                [((tm, tk), x.dtype), ((tk, tn), w.dtype), ((tm, tn), F32)]),
        name=name,
    )(x, w)


def _layer_norm_rows(y, g, b):
    mu = jnp.mean(y, axis=-1, keepdims=True)
    yc = y - mu
    var = jnp.mean(yc * yc, axis=-1, keepdims=True)
    return yc * lax.rsqrt(var + LN_EPS) * g + b


def _res_ln_kernel(x_ref, m_ref, g_ref, b_ref, o32_ref, o16_ref, *, alpha):
    out = _layer_norm_rows(alpha * x_ref[...] + m_ref[...], g_ref[...], b_ref[...])
    o32_ref[...] = out
    o16_ref[...] = out.astype(BF16)


def _res_ln(x, m, g, b, alpha, name="res_ln"):
    n, d = x.shape
    tr = _tile(n, LN_ROWS, V7X_SUBLANES * 2)
    row = pl.BlockSpec((tr, d), lambda i: (i, 0))
    vec = pl.BlockSpec((1, d), lambda i: (0, 0))
    return pl.pallas_call(
        functools.partial(_res_ln_kernel, alpha=alpha),
        grid=(n // tr,),
        in_specs=[row, row, vec, vec],
        out_specs=[row, row],
        out_shape=[jax.ShapeDtypeStruct((n, d), F32), jax.ShapeDtypeStruct((n, d), BF16)],
        compiler_params=_params(("parallel",),
                                [((tr, d), F32)] * 3 + [((tr, d), BF16)],
                                temps=2 * _nbytes((tr, d), F32)),
        name=name,
    )(x, m, g.reshape(1, d), b.reshape(1, d))


def _gate_kernel(u_ref, v_ref, g_ref, b_ref, ws_ref, bs_ref, o_ref, *, chunk, groups):
    rows, width = v_ref.shape
    gw = width // groups
    for c in range(rows // chunk):
        r0 = c * chunk
        v = _layer_norm_rows(v_ref[r0:r0 + chunk, :].astype(F32), g_ref[...], b_ref[...])
        v = v.astype(BF16)
        for g in range(groups):
            c0 = g * gw
            sv = jnp.dot(ws_ref[g], v[:, c0:c0 + gw], preferred_element_type=F32) + bs_ref[g]
            u = u_ref[r0:r0 + chunk, c0:c0 + gw].astype(F32)
            o_ref[r0:r0 + chunk, c0:c0 + gw] = (u * sv).astype(o_ref.dtype)


def _gate(z, ln_g, ln_b, w_s, b_s, name="gate"):
    n, two_w = z.shape
    width = two_w // 2
    groups, chunk, _ = w_s.shape
    gw = width // groups
    tr = _tile(n, GATE_ROWS, chunk)
    bs = jnp.broadcast_to(b_s[:, :, None], (groups, chunk, gw)).astype(F32)
    return pl.pallas_call(
        functools.partial(_gate_kernel, chunk=chunk, groups=groups),
        grid=(n // tr,),
        in_specs=[pl.BlockSpec((tr, width), lambda i: (i, 0)),
                  pl.BlockSpec((tr, width), lambda i: (i, 1)),
                  pl.BlockSpec((1, width), lambda i: (0, 0)),
                  pl.BlockSpec((1, width), lambda i: (0, 0)),
                  pl.BlockSpec((groups, chunk, chunk), lambda i: (0, 0, 0)),
                  pl.BlockSpec((groups, chunk, gw), lambda i: (0, 0, 0))],
        out_specs=pl.BlockSpec((tr, width), lambda i: (i, 0)),
        out_shape=jax.ShapeDtypeStruct((n, width), BF16),
        compiler_params=_params(("parallel",),
                                [((tr, width), BF16)] * 3 + [((groups, chunk, chunk), BF16),
                                                              ((groups, chunk, gw), F32)],
                                temps=3 * _nbytes((chunk, width), F32)),
        name=name,
    )(z, z, ln_g.reshape(1, width), ln_b.reshape(1, width), w_s, bs)


def _pool_kernel(hc_ref, hp_ref, hn_ref, wg_ref, sc_ref, o_ref, *, windows, seq_len):
    tp, width = hc_ref.shape
    gw = width // len(windows)
    ext = tp + 2 * POOL_HALO
    t0 = (pl.program_id(0) * tp) % seq_len
    first = t0 == 0
    last = t0 + tp == seq_len
    pos = t0 + lax.broadcasted_iota(jnp.int32, (tp, gw), 0)
    for g, w in enumerate(windows):
        c0 = g * gw
        left = w // 2
        right = w - 1 - left
        hc = hc_ref[:, c0:c0 + gw]
        hp = jnp.where(first, 0.0, hp_ref[:, c0:c0 + gw])
        hn = jnp.where(last, 0.0, hn_ref[:, c0:c0 + gw])
        s = jnp.concatenate([hp, hc, hn], axis=0)
        span = 1
        while span < w:
            s = s + pltpu.roll(s, ext - span, axis=0)
            span *= 2
        lead = POOL_HALO - left
        if lead:
            s = pltpu.roll(s, ext - lead, axis=0)
        wsum = s[:tp]
        cnt = (jnp.minimum(pos + right + 1, seq_len) - jnp.maximum(pos - left, 0)).astype(F32)
        p = wsum / cnt - hc
        mg = jnp.dot(p.astype(BF16), wg_ref[g], preferred_element_type=F32)
        o_ref[:, c0:c0 + gw] = (mg * sc_ref[:, c0:c0 + gw]).astype(o_ref.dtype)


def _pool(h, w_grp, scale, seq_len, name="pool"):
    n, width = h.shape
    groups, gw, _ = w_grp.shape
    assert groups == len(POOL_WINDOWS) and groups * gw == width
    for w in POOL_WINDOWS:
        assert w & (w - 1) == 0 and w // 2 <= POOL_HALO
    tp = _tile(seq_len, POOL_ROWS, V7X_SUBLANES * 2)
    per_tile = tp // POOL_HALO
    n_halo = n // POOL_HALO
    return pl.pallas_call(
        functools.partial(_pool_kernel, windows=POOL_WINDOWS, seq_len=seq_len),
        grid=(n // tp,),
        in_specs=[pl.BlockSpec((tp, width), lambda i: (i, 0)),
                  pl.BlockSpec((POOL_HALO, width),
                               lambda i: (jnp.maximum(i * per_tile - 1, 0), 0)),
                  pl.BlockSpec((POOL_HALO, width),
                               lambda i: (jnp.minimum((i + 1) * per_tile, n_halo - 1), 0)),
                  pl.BlockSpec((groups, gw, gw), lambda i: (0, 0, 0)),
                  pl.BlockSpec((1, width), lambda i: (0, 0))],
        out_specs=pl.BlockSpec((tp, width), lambda i: (i, 0)),
        out_shape=jax.ShapeDtypeStruct((n, width), BF16),
        compiler_params=_params(("parallel",),
                                [((tp, width), F32), ((groups, gw, gw), BF16), ((tp, width), BF16)],
                                temps=6 * _nbytes((tp + 2 * POOL_HALO, gw), F32)),
        name=name,
    )(h, h, h, w_grp, scale.reshape(1, width))


def _natten_kernel(q_ref, k_ref, v_ref, bias_ref, o_ref, *, n_rows, kh, heads, dh):
    half = kh // 2
    span = kh * GRID_W

    def row_body(r, carry):
        rs = jnp.clip(r - half, 0, n_rows - kh)
        rt = jnp.where(r > n_rows - kh + half, r - (n_rows - kh), jnp.minimum(r, half))
        q0 = pl.multiple_of(r * GRID_W, GRID_W)
        k0 = pl.multiple_of(rs * GRID_W, GRID_W)
        for h in range(heads):
            c0 = h * dh
            q = q_ref[pl.ds(q0, GRID_W), c0:c0 + dh]
            k = k_ref[pl.ds(k0, span), c0:c0 + dh]
            v = v_ref[pl.ds(k0, span), c0:c0 + dh]
            s = lax.dot_general(q, k, (((1,), (1,)), ((), ())), preferred_element_type=F32)
            s = s + bias_ref[h, rt]
            e = jnp.exp(s - jnp.max(s, axis=-1, keepdims=True))
            denom = jnp.sum(e, axis=-1, keepdims=True)
            o = jnp.dot(e.astype(BF16), v, preferred_element_type=F32) / denom
            o_ref[pl.ds(q0, GRID_W), c0:c0 + dh] = o.astype(o_ref.dtype)
        return carry

    lax.fori_loop(0, n_rows, row_body, 0)


def _natten_bias(rpb, n_rows, kh, win_rows_max, win_cols):
    c = np.arange(GRID_W)
    kc = np.arange(GRID_W)
    cstart = np.clip(c - win_cols // 2, 0, GRID_W - win_cols)
    col_ok = (kc[None, :] >= cstart[:, None]) & (kc[None, :] < cstart[:, None] + win_cols)
    col_off = np.clip(kc[None, :] - c[:, None] + win_cols - 1, 0, 2 * win_cols - 2)
    row_off = np.arange(kh)[None, :] - np.arange(kh)[:, None] + win_rows_max - 1
    tbl = rpb[:, row_off[:, :, None, None], col_off[None, None, :, :]]
    tbl = jnp.where(col_ok[None, None, None], tbl.astype(F32), NEG)
    tbl = jnp.transpose(tbl, (0, 1, 3, 2, 4))
    return tbl.reshape(rpb.shape[0], kh, GRID_W, kh * GRID_W)


def _natten(qkv, rpb, batch, seq_len, name="natten"):
    n, three_d = qkv.shape
    d = three_d // 3
    heads = rpb.shape[0]
    dh = d // heads
    win_rows_max = (rpb.shape[1] + 1) // 2
    win_cols = (rpb.shape[2] + 1) // 2
    n_rows = seq_len // GRID_W
    kh = min(win_rows_max, n_rows)
    hb = min(ATTN_HEADS_PER_STEP, heads)
    assert heads % hb == 0 and (hb * dh) % V7X_LANES == 0
    bias = _natten_bias(rpb, n_rows, kh, win_rows_max, win_cols)
    groups = heads // hb
    blk = (seq_len, hb * dh)
    return pl.pallas_call(
        functools.partial(_natten_kernel, n_rows=n_rows, kh=kh, heads=hb, dh=dh),
        grid=(batch, groups),
        in_specs=[pl.BlockSpec(blk, lambda b, g: (b, g)),
                  pl.BlockSpec(blk, lambda b, g: (b, groups + g)),
                  pl.BlockSpec(blk, lambda b, g: (b, 2 * groups + g)),
                  pl.BlockSpec((hb, kh, GRID_W, kh * GRID_W), lambda b, g: (g, 0, 0, 0))],
        out_specs=pl.BlockSpec(blk, lambda b, g: (b, g)),
        out_shape=jax.ShapeDtypeStruct((n, d), BF16),
        compiler_params=_params(("parallel", "arbitrary"),
                                [(blk, BF16)] * 4 + [((hb, kh, GRID_W, kh * GRID_W), F32)]),
        name=name,
    )(qkv, qkv, qkv, bias)


def _trunk(x, p):
    batch, seq_len, d = x.shape
    depth = p["ln1_g"].shape[0]
    alpha = (2 * depth) ** 0.25
    x32 = x.reshape(batch * seq_len, d)
    x16 = x32.astype(BF16)
    for i in range(depth):
        kind = i % N_MIXERS
        j = i // N_MIXERS
        if kind == 0:
            z = _matmul(x16, p["a_w_in"][j], BF16, act="gelu", name="a_in")
            gated = _gate(z, p["a_ln_g"][j], p["a_ln_b"][j], p["a_w_s"][j], p["a_b_s"][j])
            m = _matmul(gated, p["a_w_out"][j], F32, name="a_out")
        elif kind == 1:
            h = _matmul(x16, p["b_w_in"][j], F32, name="b_in")
            pooled = _pool(h, p["b_w_grp"][j], p["b_scale"][j], seq_len)
            m = _matmul(pooled, p["b_w_out"][j], F32, name="b_out")
        else:
            heads = p["c_rpb"].shape[1]
            qkv = _matmul(x16, p["c_w_qkv"][j], BF16, act="colscale", scaled_cols=d,
                          col_scale=(d // heads) ** -0.5, name="c_qkv")
            att = _natten(qkv, p["c_rpb"][j], batch, seq_len)
            m = _matmul(att, p["c_w_out"][j], F32, name="c_out")
        x32, x16 = _res_ln(x32, m, p["ln1_g"][i], p["ln1_b"][i], alpha, name="ln1")
        hid = _matmul(x16, p["ffn_w1"][i], BF16, act="relu2", name="ffn_up")
        m = _matmul_acc(hid, p["ffn_w2"][i], name="ffn_down")
        x32, x16 = _res_ln(x32, m, p["ln2_g"][i], p["ln2_b"][i], alpha, name="ln2")
    return x32.reshape(batch, seq_len, d)


_MATMUL_WEIGHTS = ("a_w_in", "a_w_s", "a_w_out", "b_w_in", "b_w_grp", "b_w_out",
                   "c_w_qkv", "c_w_out", "ffn_w1", "ffn_w2")


def kernel(x_prompt, x_sample, a_w_in, a_ln_g, a_ln_b, a_w_s, a_b_s, a_w_out, b_w_in, b_w_grp, b_scale, b_w_out, c_w_qkv, c_rpb, c_w_out, ln1_g, ln1_b, ffn_w1, ffn_w2, ln2_g, ln2_b):
    p = dict(a_w_in=a_w_in, a_ln_g=a_ln_g, a_ln_b=a_ln_b, a_w_s=a_w_s, a_b_s=a_b_s,
             a_w_out=a_w_out, b_w_in=b_w_in, b_w_grp=b_w_grp, b_scale=b_scale, b_w_out=b_w_out,
             c_w_qkv=c_w_qkv, c_rpb=c_rpb, c_w_out=c_w_out, ln1_g=ln1_g, ln1_b=ln1_b,
             ffn_w1=ffn_w1, ffn_w2=ffn_w2, ln2_g=ln2_g, ln2_b=ln2_b)
    for name in _MATMUL_WEIGHTS:
        p[name] = p[name].astype(BF16)
    return (_trunk(x_prompt, p), _trunk(x_sample, p))
```
